```python
import math
import jax, jax.numpy as jnp
from jax import lax
import numpy as np

D_MODEL = 4096
BATCH = 2
SEQ = 4096
DEPTH = 2
DEC_BATCH = 32
DEC_SEQ = 32
PAST_LEN = 1024

CHUNK = 64
N_META = 16
Q_BLOCK = 128
MIX = D_MODEL
WIDTH_A = MIX // 2
WIDTH_B = MIX - WIDTH_A
DA = 64
H_A = WIDTH_A // (2 * DA)
DB = 128
H_B = WIDTH_B // DB
D_FF = ((8 * D_MODEL + 3 * 256 - 1) // (3 * 256)) * 256
EPS = 1e-5

kernel_name = 'hymba_diff_stickbreak_stream_step'


def rmsnorm(x, g):
    xf = x.astype(jnp.float32)
    y = xf * lax.rsqrt(jnp.mean(xf * xf, axis=-1, keepdims=True) + EPS)
    return (y * g.astype(jnp.float32)).astype(x.dtype)


def alibi_slopes():
    return jnp.exp2(-8.0 * jnp.arange(1, H_A + 1, dtype=jnp.float32) / H_A)


def lambda_init(layer):
    return 0.8 - 0.6 * math.exp(-0.3 * layer)


def diff_lambda(lq1, lk1, lq2, lk2, lam_init):
    f = lambda a: a.astype(jnp.float32)
    return jnp.exp(jnp.sum(f(lq1) * f(lk1))) - jnp.exp(jnp.sum(f(lq2) * f(lk2))) + lam_init


def heads(t, n):
    return t.reshape(t.shape[:-1] + (n, t.shape[-1] // n))


def project_qkv(h, w_in_l):
    qa, ka, va, qb, kb, vb = jnp.split(h @ w_in_l, 6, axis=-1)
    return (heads(qa, H_A), heads(ka, H_A), heads(va, H_A),
            heads(qb, H_B), heads(kb, H_B), heads(vb, H_B))


def diff_attention(q, k, v, q_pos, q_chk, k_pos, k_chk, lam):
    b, nq = q.shape[:2]
    nk = k.shape[1]
    q2 = q.reshape(b, nq, H_A, 2, DA)
    k2 = k.reshape(b, nk, H_A, 2, DA)
    s = jnp.einsum('bqhcd,bkhcd->bchqk', q2, k2,
                   preferred_element_type=jnp.float32) * (DA ** -0.5)
    dist = jnp.abs(q_pos[:, None] - k_pos[None, :]).astype(jnp.float32)
    bias = -alibi_slopes()[:, None, None] * dist
    visible = k_chk[None, :] <= q_chk[:, None]
    s = jnp.where(visible, s + bias, -jnp.inf)
    p = jax.nn.softmax(s, axis=-1)
    w = p[:, 0] - lam * p[:, 1]
    return jnp.einsum('bhqk,bkhe->bqhe', w.astype(v.dtype), v)


def stick_breaking(q, k, v, q_pos, k_pos):
    z = jnp.einsum('bqhd,bkhd->bhqk', q, k,
                   preferred_element_type=jnp.float32) * (DB ** -0.5)
    earlier = k_pos[None, :] < q_pos[:, None]
    log_beta = jax.nn.log_sigmoid(z)
    log_rest = jnp.where(earlier, jax.nn.log_sigmoid(-z), 0.0)
    between = lax.cumsum(log_rest, axis=3, reverse=True) - log_rest
    a = jnp.where(earlier, jnp.exp(log_beta + between), 0.0)
    return jnp.einsum('bhqk,bkhd->bqhd', a.astype(v.dtype), v)


def mix_queries(qa, qb, ka, va, kb, vb, q_pos, q_chk, k_pos, k_chk,
                lam, lam_init, g_sub, g_sb):
    b, nq = qa.shape[:2]
    oa = rmsnorm(diff_attention(qa, ka, va, q_pos, q_chk, k_pos, k_chk, lam), g_sub) * (1.0 - lam_init)
    ob = rmsnorm(stick_breaking(qb, kb, vb, q_pos, k_pos), g_sb)
    return jnp.concatenate([oa.reshape(b, nq, WIDTH_A), ob.reshape(b, nq, WIDTH_B)], axis=-1)


def swiglu_ffn(x, g, wg, wu, wd):
    h = rmsnorm(x, g)
    return (jax.nn.silu(h @ wg) * (h @ wu)) @ wd


def setup_inputs(seed: int = 0) -> dict:
    key = jax.random.key(seed)
    ks = jax.random.split(key, 24)
    f32 = jnp.float32
    nrm = lambda k, shape, scale: jax.random.normal(k, shape, f32) * scale
    gain = lambda k, shape: 1.0 + 0.01 * jax.random.normal(k, shape, f32)
    return {
        'x_prompt': nrm(ks[0], (BATCH, SEQ, D_MODEL), 1.0),
        'x_sample': nrm(ks[1], (DEC_BATCH, DEC_SEQ, D_MODEL), 1.0),
        'cache_k_diff': nrm(ks[2], (DEPTH, DEC_BATCH, PAST_LEN, H_A, 2 * DA), 1.0),
        'cache_v_diff': nrm(ks[3], (DEPTH, DEC_BATCH, PAST_LEN, H_A, 2 * DA), 1.0),
        'cache_k_sb': nrm(ks[4], (DEPTH, DEC_BATCH, PAST_LEN, H_B, DB), 1.0),
        'cache_v_sb': nrm(ks[5], (DEPTH, DEC_BATCH, PAST_LEN, H_B, DB), 1.0),
        'meta_tokens': nrm(ks[6], (N_META, D_MODEL), 1.0),
        'g_mix': gain(ks[7], (DEPTH, D_MODEL)),
        'w_in': nrm(ks[8], (DEPTH, D_MODEL, 3 * MIX), D_MODEL ** -0.5),
        'lambda_q1': nrm(ks[9], (DEPTH, DA), 0.1),
        'lambda_k1': nrm(ks[10], (DEPTH, DA), 0.1),
        'lambda_q2': nrm(ks[11], (DEPTH, DA), 0.1),
        'lambda_k2': nrm(ks[12], (DEPTH, DA), 0.1),
        'g_sub_diff': gain(ks[13], (DEPTH, 2 * DA)),
        'g_out_sb': gain(ks[14], (DEPTH, DB)),
        'w_out': nrm(ks[15], (DEPTH, MIX, D_MODEL), MIX ** -0.5),
        'g_ffn': gain(ks[16], (DEPTH, D_MODEL)),
        'w_gate': nrm(ks[17], (DEPTH, D_MODEL, D_FF), D_MODEL ** -0.5),
        'w_up': nrm(ks[18], (DEPTH, D_MODEL, D_FF), D_MODEL ** -0.5),
        'w_down': nrm(ks[19], (DEPTH, D_FF, D_MODEL), D_FF ** -0.5),
        'g_final': gain(ks[20], (D_MODEL,)),
    }


def reference(x_prompt, x_sample, cache_k_diff, cache_v_diff, cache_k_sb, cache_v_sb,
              meta_tokens, g_mix, w_in, lambda_q1, lambda_k1, lambda_q2, lambda_k2,
              g_sub_diff, g_out_sb, w_out, g_ffn, w_gate, w_up, w_down, g_final):
    b = x_prompt.shape[0]
    meta = jnp.broadcast_to(meta_tokens[None].astype(x_prompt.dtype), (b, N_META, D_MODEL))
    xp = jnp.concatenate([meta, x_prompt], axis=1)
    L = xp.shape[1]
    nb = -(-L // Q_BLOCK)
    L_pad = nb * Q_BLOCK
    p_pos = jnp.arange(L_pad, dtype=jnp.int32)
    p_chk = jnp.where(p_pos < N_META, 0, (p_pos - N_META) // CHUNK + 1)
    k_pos_p = p_pos[:L]
    k_chk_p = p_chk[:L]

    def to_blocks(t):
        t = jnp.pad(t, ((0, 0), (0, L_pad - L), (0, 0), (0, 0)))
        return jnp.moveaxis(t.reshape((b, nb, Q_BLOCK) + t.shape[2:]), 1, 0)

    xs = x_sample
    bs, ns = xs.shape[:2]
    past = cache_k_diff.shape[2]
    s_qpos = past + jnp.arange(ns, dtype=jnp.int32)
    s_qchk = s_qpos // CHUNK
    s_kpos = jnp.arange(past + ns, dtype=jnp.int32)
    s_kchk = s_kpos // CHUNK

    kdp, vdp, ksp, vsp = [], [], [], []
    kds, vds, kss, vss = [], [], [], []
    for l in range(DEPTH):
        lam0 = lambda_init(l)
        lam = diff_lambda(lambda_q1[l], lambda_k1[l], lambda_q2[l], lambda_k2[l], lam0)

        qa, ka, va, qb, kb, vb = project_qkv(rmsnorm(xp, g_mix[l]), w_in[l])
        kdp.append(ka); vdp.append(va); ksp.append(kb); vsp.append(vb)

        def block(args, ka=ka, va=va, kb=kb, vb=vb, lam=lam, lam0=lam0, l=l):
            qa_blk, qb_blk, qp, qc = args
            return mix_queries(qa_blk, qb_blk, ka, va, kb, vb, qp, qc, k_pos_p, k_chk_p,
                               lam, lam0, g_sub_diff[l], g_out_sb[l])

        o = lax.map(block, (to_blocks(qa), to_blocks(qb),
                            p_pos.reshape(nb, Q_BLOCK), p_chk.reshape(nb, Q_BLOCK)))
        o = jnp.moveaxis(o, 0, 1).reshape(b, L_pad, MIX)[:, :L]
        xp = xp + o @ w_out[l]
        xp = xp + swiglu_ffn(xp, g_ffn[l], w_gate[l], w_up[l], w_down[l])

        qa, ka, va, qb, kb, vb = project_qkv(rmsnorm(xs, g_mix[l]), w_in[l])
        kds.append(ka); vds.append(va); kss.append(kb); vss.append(vb)
        ka_all = jnp.concatenate([cache_k_diff[l].astype(ka.dtype), ka], axis=1)
        va_all = jnp.concatenate([cache_v_diff[l].astype(va.dtype), va], axis=1)
        kb_all = jnp.concatenate([cache_k_sb[l].astype(kb.dtype), kb], axis=1)
        vb_all = jnp.concatenate([cache_v_sb[l].astype(vb.dtype), vb], axis=1)
        o = mix_queries(qa, qb, ka_all, va_all, kb_all, vb_all, s_qpos, s_qchk, s_kpos, s_kchk,
                        lam, lam0, g_sub_diff[l], g_out_sb[l])
        xs = xs + o @ w_out[l]
        xs = xs + swiglu_ffn(xs, g_ffn[l], w_gate[l], w_up[l], w_down[l])

    y_prompt = rmsnorm(xp, g_final)[:, N_META:]
    y_sample = rmsnorm(xs, g_final)
    return (y_prompt, y_sample,
            jnp.stack(kdp, 0), jnp.stack(vdp, 0), jnp.stack(ksp, 0), jnp.stack(vsp, 0),
            jnp.stack(kds, 0), jnp.stack(vds, 0), jnp.stack(kss, 0), jnp.stack(vss, 0))
```

```python
import functools
import math

import jax
import jax.numpy as jnp
from jax import lax
from jax.experimental import pallas as pl
from jax.experimental.pallas import tpu as pltpu

D_MODEL = 4096
BATCH = 2
SEQ = 4096
DEPTH = 2
DEC_BATCH = 32
DEC_SEQ = 32
PAST_LEN = 1024
CHUNK = 64
N_META = 16
N_HEADS = 16
HEAD_DIM = 128
DA = 64
D_FF = 11008
D_FF_PAD = 11264
EPS = 1e-5
NEG = -1e30

ROW_SAMPLE = BATCH * SEQ
ROW_META = ROW_SAMPLE + DEC_BATCH * DEC_SEQ
M_REAL = ROW_META + BATCH * N_META
M_PAD = 9280
TM = 928
TQ = 256
TK = 256
NQB = SEQ // TQ
HEADS_PER_STEP = 4

COL_QA, COL_KA, COL_VA, COL_QB, COL_KB, COL_VB = 0, 16, 32, 48, 64, 80

VMEM_LIMIT = 56 * 1024 * 1024

f32 = jnp.float32
bf16 = jnp.bfloat16


def _params(sem, vmem=VMEM_LIMIT):
    return pltpu.CompilerParams(dimension_semantics=sem, vmem_limit_bytes=vmem)


def _rmsnorm_kernel(x_ref, g_ref, o_ref):
    x = x_ref[...]
    y = x * lax.rsqrt(jnp.mean(x * x, axis=-1, keepdims=True) + EPS)
    o_ref[...] = (y * g_ref[...]).astype(o_ref.dtype)


def _rmsnorm(x, g, *, rows, row_block, first_block, out_dtype):
    nblk = rows // row_block
    return pl.pallas_call(
        _rmsnorm_kernel,
        grid=(nblk,),
        in_specs=[pl.BlockSpec((row_block, D_MODEL), lambda i: (first_block + i, 0)),
                  pl.BlockSpec((1, D_MODEL), lambda i: (0, 0))],
        out_specs=pl.BlockSpec((row_block, D_MODEL), lambda i: (i, 0)),
        out_shape=jax.ShapeDtypeStruct((rows, D_MODEL), out_dtype),
        compiler_params=_params(("arbitrary",)),
        name="rmsnorm",
    )(x, g.reshape(1, D_MODEL))


QKV_TN = 512
_QKV_NT = 6 * 2048 // QKV_TN
_QKV_PER = 2048 // QKV_TN


def _qkv_kernel(x_ref, w_ref, obf_ref, okv_ref):
    n = pl.program_id(1)
    acc = jnp.dot(x_ref[...], w_ref[...], preferred_element_type=f32)
    okv_ref[...] = acc
    grp = n // _QKV_PER
    scale = jnp.where(grp == 0, DA ** -0.5, jnp.where(grp == 3, HEAD_DIM ** -0.5, 1.0)).astype(f32)
    obf_ref[...] = (acc * scale).astype(bf16)


def _kv_col_block(n):
    p = _QKV_PER
    return jnp.where(n < 3 * p, jnp.maximum(n - p, 0), 2 * p + jnp.maximum(n - 4 * p, 0))


def _qkv_proj(h, w):
    return pl.pallas_call(
        _qkv_kernel,
        grid=(M_PAD // TM, _QKV_NT),
        in_specs=[pl.BlockSpec((TM, D_MODEL), lambda m, n: (m, 0)),
                  pl.BlockSpec((D_MODEL, QKV_TN), lambda m, n: (0, n))],
        out_specs=[pl.BlockSpec((TM, QKV_TN), lambda m, n: (m, n)),
                   pl.BlockSpec((TM, QKV_TN), lambda m, n: (m, _kv_col_block(n)))],
        out_shape=[jax.ShapeDtypeStruct((M_PAD, 6 * 2048), bf16),
                   jax.ShapeDtypeStruct((M_PAD, 4 * 2048), f32)],
        compiler_params=_params(("arbitrary", "arbitrary")),
        name="qkv_proj",
    )(h, w)


def _resid_matmul_kernel(x_ref, w_ref, r_ref, o_ref):
    o_ref[...] = r_ref[...] + jnp.dot(x_ref[...], w_ref[...], preferred_element_type=f32)


def _resid_matmul(x, w, resid, *, tn, single_buffer_x):
    k = x.shape[1]
    n_out = w.shape[1]
    x_kwargs = dict(pipeline_mode=pl.Buffered(1)) if single_buffer_x else {}
    return pl.pallas_call(
        _resid_matmul_kernel,
        grid=(M_PAD // TM, n_out // tn),
        in_specs=[pl.BlockSpec((TM, k), lambda m, n: (m, 0), **x_kwargs),
                  pl.BlockSpec((k, tn), lambda m, n: (0, n)),
                  pl.BlockSpec((TM, tn), lambda m, n: (m, n))],
        out_specs=pl.BlockSpec((TM, tn), lambda m, n: (m, n)),
        out_shape=jax.ShapeDtypeStruct((M_PAD, n_out), f32),
        input_output_aliases={2: 0},
        compiler_params=_params(("arbitrary", "arbitrary")),
        name="resid_matmul",
    )(x, w, resid)


GU_TN = 512


def _gateup_kernel(x_ref, wg_ref, wu_ref, o_ref):
    x = x_ref[...]
    g = jnp.dot(x, wg_ref[...], preferred_element_type=f32)
    u = jnp.dot(x, wu_ref[...], preferred_element_type=f32)
    o_ref[...] = (g * jax.nn.sigmoid(g) * u).astype(bf16)


def _gateup(h, wg, wu):
    return pl.pallas_call(
        _gateup_kernel,
        grid=(M_PAD // TM, D_FF_PAD // GU_TN),
        in_specs=[pl.BlockSpec((TM, D_MODEL), lambda m, n: (m, 0)),
                  pl.BlockSpec((D_MODEL, GU_TN), lambda m, n: (0, n)),
                  pl.BlockSpec((D_MODEL, GU_TN), lambda m, n: (0, n))],
        out_specs=pl.BlockSpec((TM, GU_TN), lambda m, n: (m, n)),
        out_shape=jax.ShapeDtypeStruct((M_PAD, D_FF_PAD), bf16),
        compiler_params=_params(("arbitrary", "arbitrary")),
        name="gateup",
    )(h, wg, wu)


def _nt_dot(a, b):
    return lax.dot_general(a, b, (((1,), (1,)), ((), ())), preferred_element_type=f32)


def _diff_lambda(lamv, lam0):
    a = jnp.sum(lamv[0:1] * lamv[1:2], axis=-1, keepdims=True)
    b = jnp.sum(lamv[2:3] * lamv[3:4], axis=-1, keepdims=True)
    return jnp.exp(a) - jnp.exp(b) + lam0


def _split_q(q):
    lane = lax.broadcasted_iota(jnp.int32, q.shape, 1)
    zero = jnp.zeros_like(q)
    return jnp.where(lane < DA, q, zero), jnp.where(lane >= DA, q, zero)


def _softmax_block(t, rowterm, v, state):
    mt = jnp.max(t, axis=-1, keepdims=True)
    if rowterm is not None:
        mt = mt + rowterm
    if state is None:
        m_new = mt
    else:
        m_old, l_old, acc_old = state
        m_new = jnp.maximum(m_old, mt)
    shift = -m_new if rowterm is None else rowterm - m_new
    p = jnp.exp(t + shift)
    ps = jnp.sum(p, axis=-1, keepdims=True)
    pv = jnp.dot(p.astype(bf16), v, preferred_element_type=f32)
    if state is None:
        return m_new, ps, pv
    alpha = jnp.exp(m_old - m_new)
    return m_new, alpha * l_old + ps, alpha * acc_old + pv


def _diff_finish(states, lam, lam0, g):
    (_, l1, a1), (_, l2, a2) = states
    o = a1 / l1 - lam * (a2 / l2)
    y = o * lax.rsqrt(jnp.mean(o * o, axis=-1, keepdims=True) + EPS)
    return ((y * g) * (1.0 - lam0)).astype(bf16)


def _strict_lower(n):
    j = lax.broadcasted_iota(jnp.int32, (n, n), 0)
    s = lax.broadcasted_iota(jnp.int32, (n, n), 1)
    return jnp.where(j > s, 1.0, 0.0).astype(bf16)


def _sb_block(z, v, tri, carry, acc, earlier):
    soft = jnp.log(1.0 + jnp.exp(-jnp.abs(z)))
    log_beta = jnp.minimum(z, 0.0) - soft
    log_rest = log_beta - z
    if earlier is not None:
        log_rest = jnp.where(earlier, log_rest, 0.0)
    hi = log_rest.astype(bf16)
    lo = (log_rest - hi.astype(f32)).astype(bf16)
    between = (jnp.dot(hi, tri, preferred_element_type=f32)
               + jnp.dot(lo, tri, preferred_element_type=f32))
    arg = log_beta + between
    if carry is not None:
        arg = arg + carry
    if earlier is not None:
        arg = jnp.where(earlier, arg, NEG)
    a = jnp.exp(arg)
    pv = jnp.dot(a.astype(bf16), v, preferred_element_type=f32)
    rest = jnp.sum(log_rest, axis=-1, keepdims=True)
    new_carry = rest if carry is None else carry + rest
    new_acc = pv if acc is None else acc + pv
    return new_carry, new_acc


def _sb_finish(acc, g):
    y = acc * lax.rsqrt(jnp.mean(acc * acc, axis=-1, keepdims=True) + EPS)
    return (y * g).astype(bf16)


def _diff_prompt_kernel(slopes_ref, lamv_ref, g_ref, q_ref, k_ref, v_ref, mk_ref, mv_ref, oin_ref,
                        o_ref, m_sc, l_sc, acc_sc, bias_sc, *, lam0):
    del oin_ref
    h = pl.program_id(1)
    qi = pl.program_id(2)
    slope = slopes_ref[h]
    qs = _split_q(q_ref[...])

    r = lax.broadcasted_iota(jnp.int32, (TQ, TK), 0)
    c = lax.broadcasted_iota(jnp.int32, (TQ, TK), 1)
    rc = (r - c).astype(f32)
    bias_sc[...] = -slope * rc
    shift = CHUNK.bit_length() - 1
    visible = lax.shift_right_logical(c, shift) <= lax.shift_right_logical(r, shift)
    bias_diag = jnp.where(visible, -slope * jnp.abs(rc), NEG)

    q0 = pl.multiple_of(qi * TQ, TQ)
    k = k_ref[pl.ds(q0, TK), :]
    v = v_ref[pl.ds(q0, TK), :]
    for i in range(2):
        m, l, acc = _softmax_block(_nt_dot(qs[i], k) + bias_diag, None, v, None)
        m_sc[i] = m
        l_sc[i] = l
        acc_sc[i] = acc

    def body(ki, carry):
        k0 = pl.multiple_of(ki * TK, TK)
        kb = k_ref[pl.ds(k0, TK), :]
        vb = v_ref[pl.ds(k0, TK), :]
        steps = jnp.full((TQ, 1), (qi - ki) * TQ, jnp.int32).astype(f32)
        rowterm = -slope * steps
        for i in range(2):
            t = _nt_dot(qs[i], kb) + bias_sc[...]
            m, l, acc = _softmax_block(t, rowterm, vb, (m_sc[i], l_sc[i], acc_sc[i]))
            m_sc[i] = m
            l_sc[i] = l
            acc_sc[i] = acc
        return carry

    lax.fori_loop(0, qi, body, 0)

    rm = lax.broadcasted_iota(jnp.int32, (TQ, N_META), 0)
    cm = lax.broadcasted_iota(jnp.int32, (TQ, N_META), 1)
    base = jnp.full((TQ, N_META), qi * TQ + N_META, jnp.int32)
    bias_meta = -slope * (base + rm - cm).astype(f32)
    mk = mk_ref[...]
    mv = mv_ref[...]
    states = []
    for i in range(2):
        t = _nt_dot(qs[i], mk) + bias_meta
        states.append(_softmax_block(t, None, mv, (m_sc[i], l_sc[i], acc_sc[i])))

    lam = _diff_lambda(lamv_ref[...], lam0)
    o_ref[...] = _diff_finish(states, lam, lam0, g_ref[...])


def _sb_prompt_kernel(g_ref, q_ref, k_ref, v_ref, mk_ref, mv_ref, oin_ref, o_ref, carry_sc, acc_sc):
    del oin_ref
    qi = pl.program_id(2)
    q = q_ref[...]
    tri = _strict_lower(TK)
    r = lax.broadcasted_iota(jnp.int32, (TQ, TK), 0)
    c = lax.broadcasted_iota(jnp.int32, (TQ, TK), 1)

    q0 = pl.multiple_of(qi * TQ, TQ)
    carry, acc = _sb_block(_nt_dot(q, k_ref[pl.ds(q0, TK), :]), v_ref[pl.ds(q0, TK), :], tri,
                           None, None, c < r)
    carry_sc[...] = carry
    acc_sc[...] = acc

    def body(j, loop_carry):
        ki = qi - 1 - j
        k0 = pl.multiple_of(ki * TK, TK)
        carry, acc = _sb_block(_nt_dot(q, k_ref[pl.ds(k0, TK), :]), v_ref[pl.ds(k0, TK), :], tri,
                               carry_sc[...], acc_sc[...], None)
        carry_sc[...] = carry
        acc_sc[...] = acc
        return loop_carry

    lax.fori_loop(0, qi, body, 0)

    _, acc = _sb_block(_nt_dot(q, mk_ref[...]), mv_ref[...], _strict_lower(N_META),
                       carry_sc[...], acc_sc[...], None)
    o_ref[...] = _sb_finish(acc, g_ref[...])


def _prompt_specs(col_q, col_k, col_v, col_o):
    meta_blk = ROW_META // N_META
    in_specs = [
        pl.BlockSpec((TQ, HEAD_DIM), lambda b, h, qi, *_: (b * NQB + qi, col_q + h)),
        pl.BlockSpec((SEQ, HEAD_DIM), lambda b, h, qi, *_: (b, col_k + h)),
        pl.BlockSpec((SEQ, HEAD_DIM), lambda b, h, qi, *_: (b, col_v + h)),
        pl.BlockSpec((N_META, HEAD_DIM), lambda b, h, qi, *_: (meta_blk + b, col_k + h)),
        pl.BlockSpec((N_META, HEAD_DIM), lambda b, h, qi, *_: (meta_blk + b, col_v + h)),
        pl.BlockSpec(memory_space=pl.ANY),
    ]
    out_spec = pl.BlockSpec((TQ, HEAD_DIM), lambda b, h, qi, *_: (b * NQB + qi, col_o + h))
    return in_specs, out_spec


def _diff_prompt(qkv, o, slopes, lamv, g, lam0):
    in_specs, out_spec = _prompt_specs(COL_QA, COL_KA, COL_VA, 0)
    in_specs = [pl.BlockSpec((8, HEAD_DIM), lambda b, h, qi, *_: (0, 0)),
                pl.BlockSpec((1, HEAD_DIM), lambda b, h, qi, *_: (0, 0))] + in_specs
    return pl.pallas_call(
        functools.partial(_diff_prompt_kernel, lam0=lam0),
        grid_spec=pltpu.PrefetchScalarGridSpec(
            num_scalar_prefetch=1,
            grid=(BATCH, N_HEADS, NQB),
            in_specs=in_specs,
            out_specs=out_spec,
            scratch_shapes=[pltpu.VMEM((2, TQ, 1), f32), pltpu.VMEM((2, TQ, 1), f32),
                            pltpu.VMEM((2, TQ, HEAD_DIM), f32), pltpu.VMEM((TQ, TK), f32)]),
        out_shape=jax.ShapeDtypeStruct(o.shape, o.dtype),
        input_output_aliases={8: 0},
        compiler_params=_params(("arbitrary", "arbitrary", "arbitrary")),
        name="diff_prompt",
    )(slopes, lamv, g, qkv, qkv, qkv, qkv, qkv, o)


def _sb_prompt(qkv, o, g):
    in_specs, out_spec = _prompt_specs(COL_QB, COL_KB, COL_VB, N_HEADS)
    in_specs = [pl.BlockSpec((1, HEAD_DIM), lambda b, h, qi: (0, 0))] + in_specs
    return pl.pallas_call(
        _sb_prompt_kernel,
        grid=(BATCH, N_HEADS, NQB),
        in_specs=in_specs,
        out_specs=out_spec,
        scratch_shapes=[pltpu.VMEM((TQ, 1), f32), pltpu.VMEM((TQ, HEAD_DIM), f32)],
        out_shape=jax.ShapeDtypeStruct(o.shape, o.dtype),
        input_output_aliases={6: 0},
        compiler_params=_params(("arbitrary", "arbitrary", "arbitrary")),
        name="sb_prompt",
    )(g, qkv, qkv, qkv, qkv, qkv, o)


def _diff_small_kernel(slopes_ref, lamv_ref, g_ref, q_ref, k_ref, v_ref, *rest, nq, n_cache, lam0):
    if n_cache:
        ck_ref, cv_ref, _, o_ref = rest
    else:
        _, o_ref = rest
    hg = pl.program_id(1)
    lam = _diff_lambda(lamv_ref[...], lam0)
    g = g_ref[...]
    r = lax.broadcasted_iota(jnp.int32, (nq, nq), 0)
    c = lax.broadcasted_iota(jnp.int32, (nq, nq), 1)
    dist_new = jnp.abs(r - c).astype(f32)
    rk = lax.broadcasted_iota(jnp.int32, (nq, TK), 0)
    ck = lax.broadcasted_iota(jnp.int32, (nq, TK), 1)
    rc_cache = (rk - ck).astype(f32)
    for j in range(HEADS_PER_STEP):
        cols = slice(j * HEAD_DIM, (j + 1) * HEAD_DIM)
        slope = slopes_ref[hg * HEADS_PER_STEP + j]
        qs = _split_q(q_ref[:, cols])
        k = k_ref[:, cols]
        v = v_ref[:, cols]
        bias_new = -slope * dist_new
        states = [_softmax_block(_nt_dot(qs[i], k) + bias_new, None, v, None) for i in range(2)]
        for blk in range(n_cache):
            kc = ck_ref[blk * TK:(blk + 1) * TK, cols].astype(bf16)
            vc = cv_ref[blk * TK:(blk + 1) * TK, cols].astype(bf16)
            bias = -slope * (rc_cache + float(n_cache * TK - blk * TK))
            states = [_softmax_block(_nt_dot(qs[i], kc) + bias, None, vc, states[i]) for i in range(2)]
        o_ref[:, cols] = _diff_finish(states, lam, lam0, g)


def _sb_small_kernel(g_ref, q_ref, k_ref, v_ref, *rest, nq, n_cache):
    if n_cache:
        ck_ref, cv_ref, _, o_ref = rest
    else:
        _, o_ref = rest
    g = g_ref[...]
    r = lax.broadcasted_iota(jnp.int32, (nq, nq), 0)
    c = lax.broadcasted_iota(jnp.int32, (nq, nq), 1)
    earlier = c < r
    tri_new = _strict_lower(nq)
    tri = _strict_lower(TK) if n_cache else None
    for j in range(HEADS_PER_STEP):
        cols = slice(j * HEAD_DIM, (j + 1) * HEAD_DIM)
        q = q_ref[:, cols]
        carry, acc = _sb_block(_nt_dot(q, k_ref[:, cols]), v_ref[:, cols], tri_new, None, None, earlier)
        for blk in reversed(range(n_cache)):
            kc = ck_ref[blk * TK:(blk + 1) * TK, cols].astype(bf16)
            vc = cv_ref[blk * TK:(blk + 1) * TK, cols].astype(bf16)
            carry, acc = _sb_block(_nt_dot(q, kc), vc, tri, carry, acc, None)
        o_ref[:, cols] = _sb_finish(acc, g)


def _small_specs(nq, first_row, n_cache, layer, col_q, col_k, col_v, col_o):
    w = HEADS_PER_STEP * HEAD_DIM
    per = w // HEAD_DIM
    blk0 = first_row // nq
    in_specs = [
        pl.BlockSpec((nq, w), lambda i, hg, *_: (blk0 + i, col_q // per + hg)),
        pl.BlockSpec((nq, w), lambda i, hg, *_: (blk0 + i, col_k // per + hg)),
        pl.BlockSpec((nq, w), lambda i, hg, *_: (blk0 + i, col_v // per + hg)),
    ]
    if n_cache:
        in_specs += [
            pl.BlockSpec((PAST_LEN, w), lambda i, hg, *_: (layer * DEC_BATCH + i, hg)),
            pl.BlockSpec((PAST_LEN, w), lambda i, hg, *_: (layer * DEC_BATCH + i, hg)),
        ]
    in_specs.append(pl.BlockSpec(memory_space=pl.ANY))
    out_spec = pl.BlockSpec((nq, w), lambda i, hg, *_: (blk0 + i, col_o // per + hg))
    return in_specs, out_spec


def _diff_small(qkv, o, slopes, lamv, g, lam0, *, nq, items, first_row, layer, cache_k=None, cache_v=None):
    n_cache = 0 if cache_k is None else PAST_LEN // TK
    in_specs, out_spec = _small_specs(nq, first_row, n_cache, layer, COL_QA, COL_KA, COL_VA, 0)
    in_specs = [pl.BlockSpec((8, HEAD_DIM), lambda i, hg, *_: (0, 0)),
                pl.BlockSpec((1, HEAD_DIM), lambda i, hg, *_: (0, 0))] + in_specs
    args = [slopes, lamv, g, qkv, qkv, qkv] + ([cache_k, cache_v] if n_cache else []) + [o]
    return pl.pallas_call(
        functools.partial(_diff_small_kernel, nq=nq, n_cache=n_cache, lam0=lam0),
        grid_spec=pltpu.PrefetchScalarGridSpec(
            num_scalar_prefetch=1,
            grid=(items, N_HEADS // HEADS_PER_STEP),
            in_specs=in_specs,
            out_specs=out_spec),
        out_shape=jax.ShapeDtypeStruct(o.shape, o.dtype),
        input_output_aliases={len(args) - 1: 0},
        compiler_params=_params(("arbitrary", "arbitrary")),
        name="diff_small",
    )(*args)


def _sb_small(qkv, o, g, *, nq, items, first_row, layer, cache_k=None, cache_v=None):
    n_cache = 0 if cache_k is None else PAST_LEN // TK
    in_specs, out_spec = _small_specs(nq, first_row, n_cache, layer, COL_QB, COL_KB, COL_VB, N_HEADS)
    in_specs = [pl.BlockSpec((1, HEAD_DIM), lambda i, hg: (0, 0))] + in_specs
    args = [g, qkv, qkv, qkv] + ([cache_k, cache_v] if n_cache else []) + [o]
    return pl.pallas_call(
        functools.partial(_sb_small_kernel, nq=nq, n_cache=n_cache),
        grid=(items, N_HEADS // HEADS_PER_STEP),
        in_specs=in_specs,
        out_specs=out_spec,
        out_shape=jax.ShapeDtypeStruct(o.shape, o.dtype),
        input_output_aliases={len(args) - 1: 0},
        compiler_params=_params(("arbitrary", "arbitrary")),
        name="sb_small",
    )(*args)


def _alibi_slopes():
    return jnp.exp2(-8.0 * jnp.arange(1, N_HEADS + 1, dtype=f32) / N_HEADS)


def _lambda_init(layer):
    return 0.8 - 0.6 * math.exp(-0.3 * layer)


def _pad_lane(vecs):
    rows = jnp.stack([v.astype(f32) for v in vecs], axis=0)
    return jnp.pad(rows, ((0, 8 - rows.shape[0]), (0, HEAD_DIM - rows.shape[1])))


def kernel(x_prompt, x_sample, cache_k_diff, cache_v_diff, cache_k_sb, cache_v_sb, meta_tokens, g_mix, w_in, lambda_q1, lambda_k1, lambda_q2, lambda_k2, g_sub_diff, g_out_sb, w_out, g_ffn, w_gate, w_up, w_down, g_final):
    meta = meta_tokens.astype(f32)
    x = jnp.concatenate([
        x_prompt.reshape(ROW_SAMPLE, D_MODEL),
        x_sample.reshape(DEC_BATCH * DEC_SEQ, D_MODEL),
        meta, meta,
        jnp.zeros((M_PAD - M_REAL, D_MODEL), f32)], axis=0)

    slopes = _alibi_slopes()
    width = N_HEADS * HEAD_DIM
    caches = [c.reshape(DEPTH * DEC_BATCH * PAST_LEN, width)
              for c in (cache_k_diff, cache_v_diff, cache_k_sb, cache_v_sb)]
    ff_pad = D_FF_PAD - D_FF

    kv_layers = []
    for l in range(DEPTH):
        lam0 = _lambda_init(l)
        lamv = _pad_lane([lambda_q1[l], lambda_k1[l], lambda_q2[l], lambda_k2[l]])
        g_a = g_sub_diff[l].astype(f32).reshape(1, HEAD_DIM)
        g_b = g_out_sb[l].astype(f32).reshape(1, HEAD_DIM)
        w_in_l = w_in[l].astype(bf16)
        w_out_l = w_out[l].astype(bf16)
        w_gate_l = jnp.pad(w_gate[l].astype(bf16), ((0, 0), (0, ff_pad)))
        w_up_l = jnp.pad(w_up[l].astype(bf16), ((0, 0), (0, ff_pad)))
        w_down_l = jnp.pad(w_down[l].astype(bf16), ((0, ff_pad), (0, 0)))

        h = _rmsnorm(x, g_mix[l], rows=M_PAD, row_block=464, first_block=0, out_dtype=bf16)
        qkv, kv = _qkv_proj(h, w_in_l)
        kv_layers.append(kv)

        o = jnp.zeros((M_PAD, 2 * width), bf16)
        o = _diff_prompt(qkv, o, slopes, lamv, g_a, lam0)
        o = _sb_prompt(qkv, o, g_b)
        o = _diff_small(qkv, o, slopes, lamv, g_a, lam0, nq=DEC_SEQ, items=DEC_BATCH, first_row=ROW_SAMPLE,
                        layer=l, cache_k=caches[0], cache_v=caches[1])
        o = _sb_small(qkv, o, g_b, nq=DEC_SEQ, items=DEC_BATCH, first_row=ROW_SAMPLE,
                      layer=l, cache_k=caches[2], cache_v=caches[3])
        o = _diff_small(qkv, o, slopes, lamv, g_a, lam0, nq=N_META, items=BATCH, first_row=ROW_META, layer=l)
        o = _sb_small(qkv, o, g_b, nq=N_META, items=BATCH, first_row=ROW_META, layer=l)

        x = _resid_matmul(o, w_out_l, x, tn=512, single_buffer_x=False)
        h = _rmsnorm(x, g_ffn[l], rows=M_PAD, row_block=464, first_block=0, out_dtype=bf16)
        act = _gateup(h, w_gate_l, w_up_l)
        x = _resid_matmul(act, w_down_l, x, tn=256, single_buffer_x=True)

    y_prompt = _rmsnorm(x, g_final, rows=ROW_SAMPLE, row_block=256, first_block=0, out_dtype=f32)
    y_sample = _rmsnorm(x, g_final, rows=DEC_BATCH * DEC_SEQ, row_block=256,
                        first_block=ROW_SAMPLE // 256, out_dtype=f32)

    def prompt_rows(kv, group):
        cols = kv[:, group * width:(group + 1) * width]
        per_batch = [jnp.concatenate([cols[ROW_META + b * N_META:ROW_META + (b + 1) * N_META],
                                      cols[b * SEQ:(b + 1) * SEQ]], axis=0) for b in range(BATCH)]
        return jnp.stack(per_batch, 0).reshape(BATCH, N_META + SEQ, N_HEADS, HEAD_DIM)

    def sample_rows(kv, group):
        cols = kv[ROW_SAMPLE:ROW_META, group * width:(group + 1) * width]
        return cols.reshape(DEC_BATCH, DEC_SEQ, N_HEADS, HEAD_DIM)

    prompt_out = [jnp.stack([prompt_rows(kv, grp) for kv in kv_layers], 0) for grp in range(4)]
    sample_out = [jnp.stack([sample_rows(kv, grp) for kv in kv_layers], 0) for grp in range(4)]
    return (y_prompt.reshape(BATCH, SEQ, D_MODEL),
            y_sample.reshape(DEC_BATCH, DEC_SEQ, D_MODEL),
            *prompt_out, *sample_out)
```
